```python
import math
import jax, jax.numpy as jnp
from jax import lax
import numpy as np

D_MODEL = 2048
BATCH = 2
SEQ = 8192
DEPTH = 4
DEC_BATCH = 32
DEC_SEQ = 16
PAST_LEN = 4096

CHUNK = 64
HG_HEADS = 8
HG_DK = 128
HG_DV = 128
HG_DIM = HG_HEADS * HG_DK
LRU_DIM = 1024
LRU_BLOCKS = 4
LRU_BW = LRU_DIM // LRU_BLOCKS
CONV_W = 4
C_LRU = 8.0
MIX_DIM = HG_HEADS * HG_DV + LRU_DIM
IN_DIM = 4 * HG_DIM + 2 * LRU_DIM
FFN_DIM = ((8 * D_MODEL // 3 + 255) // 256) * 256
EPS = 1e-6

kernel_name = "hgrn2_rglru_parallel_hybrid_step"


def rmsnorm(x, g):
    xf = x.astype(jnp.float32)
    y = xf * lax.rsqrt(jnp.mean(xf * xf, axis=-1, keepdims=True) + EPS)
    return (y * g.astype(jnp.float32)).astype(x.dtype)


def hgrn2_chunk(S, q, k, v, g):
    T = q.shape[1]
    G = jnp.cumsum(g, axis=1)
    mask = jnp.tril(jnp.ones((T, T), dtype=bool))[None, :, :, None, None]
    diff = G[:, :, None] - G[:, None, :]
    decay = jnp.exp(jnp.where(mask, diff, -jnp.inf))
    A = jnp.einsum('bthd,bshd,btshd->bhts', q, k, decay)
    intra = jnp.einsum('bhts,bshv->bthv', A, v)
    inter = jnp.einsum('bthd,bhdv->bthv', q * jnp.exp(G), S)
    GC = G[:, -1]
    S_new = jnp.exp(GC)[..., None] * S + jnp.einsum('bshd,bshv->bhdv', k * jnp.exp(GC[:, None] - G), v)
    return S_new, intra + inter


def hgrn2_mixer(q, k, v, g, S0):
    B, T = q.shape[0], q.shape[1]
    if T <= CHUNK:
        S, o = hgrn2_chunk(S0, q, k, v, g)
        return o, S
    n = T // CHUNK

    def blk(a):
        return a.reshape((B, n, CHUNK) + a.shape[2:]).swapaxes(0, 1)

    S, o = lax.scan(lambda S, xs: hgrn2_chunk(S, *xs), S0, (blk(q), blk(k), blk(v), blk(g)))
    o = o.swapaxes(0, 1).reshape(B, T, HG_HEADS, HG_DV)
    return o, S


def causal_dwconv(x, buf, w, b):
    T = x.shape[1]
    xp = jnp.concatenate([buf.astype(x.dtype), x], axis=1)
    y = b + sum(xp[:, j:j + T] * w[j] for j in range(CONV_W))
    return y, xp[:, -(CONV_W - 1):]


def _lin_combine(left, right):
    a1, b1 = left
    a2, b2 = right
    return a1 * a2, a2 * b1 + b2


def rglru(x, w_r, b_r, w_i, b_i, lam, h0, pos0):
    B, T, _ = x.shape
    xb = x.reshape(B, T, LRU_BLOCKS, LRU_BW)
    r = jax.nn.sigmoid(jnp.einsum('btni,nij->btnj', xb, w_r).reshape(B, T, LRU_DIM) + b_r)
    i = jax.nn.sigmoid(jnp.einsum('btni,nij->btnj', xb, w_i).reshape(B, T, LRU_DIM) + b_i)
    log_a = -C_LRU * r * jax.nn.softplus(-lam)
    a = jnp.exp(log_a)
    mult = jnp.sqrt(-jnp.expm1(2.0 * log_a))
    reset = ((pos0 + jnp.arange(T)) == 0)[None, :, None]
    mult = jnp.where(reset, 1.0, mult)
    a = jnp.where(reset, 0.0, a)
    bx = x * i * mult
    A, Bc = lax.associative_scan(_lin_combine, (a, bx), axis=1)
    h = A * h0[:, None] + Bc
    return h, h[:, -1]


def layer(x, S0, h0, cbuf, pos0, lb, norm1_g, w_in, hg_norm_g, conv_w, conv_b,
          w_rgate, b_rgate, w_igate, b_igate, lru_lambda, w_out, norm2_g,
          w_ffn_gate, w_ffn_up, w_ffn_down):
    B, T, _ = x.shape
    hn = rmsnorm(x, norm1_g)
    z = (hn @ w_in).astype(jnp.float32)
    o1, o2, o3, o4, o5 = HG_DIM, 2 * HG_DIM, 3 * HG_DIM, 4 * HG_DIM, 4 * HG_DIM + LRU_DIM
    zq, zf, zi, zg, zx, zy = z[..., :o1], z[..., o1:o2], z[..., o2:o3], z[..., o3:o4], z[..., o4:o5], z[..., o5:]

    f = lb + (1.0 - lb) * jax.nn.sigmoid(zf)
    q = (zq * HG_DK ** -0.5).reshape(B, T, HG_HEADS, HG_DK)
    k = (1.0 - f).reshape(B, T, HG_HEADS, HG_DK)
    g = jnp.log(f).reshape(B, T, HG_HEADS, HG_DK)
    v = zi.reshape(B, T, HG_HEADS, HG_DV)
    o, S_new = hgrn2_mixer(q, k, v, g, S0.astype(jnp.float32))
    o = o * lax.rsqrt(jnp.mean(o * o, axis=-1, keepdims=True) + EPS)
    o = o * hg_norm_g.astype(jnp.float32).reshape(HG_HEADS, HG_DV) * jax.nn.silu(zg.reshape(B, T, HG_HEADS, HG_DV))
    o = o.reshape(B, T, HG_HEADS * HG_DV)

    xc, cbuf_new = causal_dwconv(zx, cbuf.astype(jnp.float32), conv_w.astype(jnp.float32), conv_b.astype(jnp.float32))
    hseq, h_last = rglru(xc, w_rgate.astype(jnp.float32), b_rgate.astype(jnp.float32),
                         w_igate.astype(jnp.float32), b_igate.astype(jnp.float32),
                         lru_lambda.astype(jnp.float32), h0.astype(jnp.float32), pos0)
    r_out = hseq * jax.nn.gelu(zy)

    mix = jnp.concatenate([o, r_out], axis=-1).astype(x.dtype) @ w_out
    x = x + mix
    h2 = rmsnorm(x, norm2_g)
    x = x + (jax.nn.silu(h2 @ w_ffn_gate) * (h2 @ w_ffn_up)) @ w_ffn_down
    return x, S_new, h_last, cbuf_new


def setup_inputs(seed: int = 0) -> dict:
    key = jax.random.key(seed)
    ks = jax.random.split(key, 24)

    def nrm(k, shape, s):
        return jax.random.normal(k, shape, jnp.float32) * s

    a_base = jax.random.uniform(ks[15], (DEPTH, LRU_DIM), jnp.float32, 0.9, 0.999) ** (1.0 / C_LRU)
    return {
        "x_prompt": nrm(ks[0], (BATCH, SEQ, D_MODEL), 1.0),
        "x_sample": nrm(ks[1], (DEC_BATCH, DEC_SEQ, D_MODEL), 1.0),
        "state_hgrn": nrm(ks[2], (DEPTH, DEC_BATCH, HG_HEADS, HG_DK, HG_DV), 0.5),
        "state_rglru": nrm(ks[3], (DEPTH, DEC_BATCH, LRU_DIM), 0.5),
        "state_conv": nrm(ks[4], (DEPTH, DEC_BATCH, CONV_W - 1, LRU_DIM), 1.0),
        "norm1_g": 1.0 + nrm(ks[5], (DEPTH, D_MODEL), 0.02),
        "w_in": nrm(ks[6], (DEPTH, D_MODEL, IN_DIM), D_MODEL ** -0.5),
        "lb_raw": nrm(ks[7], (DEPTH, HG_DIM), 1.0),
        "hg_norm_g": 1.0 + nrm(ks[8], (DEPTH, HG_HEADS * HG_DV), 0.02),
        "conv_w": nrm(ks[9], (DEPTH, CONV_W, LRU_DIM), CONV_W ** -0.5),
        "conv_b": nrm(ks[10], (DEPTH, LRU_DIM), 0.02),
        "w_rgate": nrm(ks[11], (DEPTH, LRU_BLOCKS, LRU_BW, LRU_BW), LRU_BW ** -0.5),
        "b_rgate": nrm(ks[12], (DEPTH, LRU_DIM), 0.02),
        "w_igate": nrm(ks[13], (DEPTH, LRU_BLOCKS, LRU_BW, LRU_BW), LRU_BW ** -0.5),
        "b_igate": nrm(ks[14], (DEPTH, LRU_DIM), 0.02),
        "lru_lambda": jnp.log(a_base) - jnp.log1p(-a_base),
        "w_out": nrm(ks[16], (DEPTH, MIX_DIM, D_MODEL), MIX_DIM ** -0.5),
        "norm2_g": 1.0 + nrm(ks[17], (DEPTH, D_MODEL), 0.02),
        "w_ffn_gate": nrm(ks[18], (DEPTH, D_MODEL, FFN_DIM), D_MODEL ** -0.5),
        "w_ffn_up": nrm(ks[19], (DEPTH, D_MODEL, FFN_DIM), D_MODEL ** -0.5),
        "w_ffn_down": nrm(ks[20], (DEPTH, FFN_DIM, D_MODEL), FFN_DIM ** -0.5),
        "final_norm_g": 1.0 + nrm(ks[21], (D_MODEL,), 0.02),
    }


def reference(x_prompt, x_sample, state_hgrn, state_rglru, state_conv, norm1_g, w_in, lb_raw,
              hg_norm_g, conv_w, conv_b, w_rgate, b_rgate, w_igate, b_igate, lru_lambda,
              w_out, norm2_g, w_ffn_gate, w_ffn_up, w_ffn_down, final_norm_g):
    lb_cum = jnp.cumsum(jax.nn.softmax(lb_raw.astype(jnp.float32), axis=0), axis=0)
    lb_all = lb_cum - lb_cum[0]

    Bp = x_prompt.shape[0]
    xp, xs = x_prompt, x_sample
    S_p = jnp.zeros((Bp, HG_HEADS, HG_DK, HG_DV), jnp.float32)
    h_p = jnp.zeros((Bp, LRU_DIM), jnp.float32)
    c_p = jnp.zeros((Bp, CONV_W - 1, LRU_DIM), jnp.float32)
    hg_p, lr_p, cv_p, hg_s, lr_s, cv_s = [], [], [], [], [], []
    for l in range(DEPTH):
        w = (lb_all[l], norm1_g[l], w_in[l], hg_norm_g[l], conv_w[l], conv_b[l], w_rgate[l], b_rgate[l],
             w_igate[l], b_igate[l], lru_lambda[l], w_out[l], norm2_g[l], w_ffn_gate[l], w_ffn_up[l], w_ffn_down[l])
        xp, S1, h1, c1 = layer(xp, S_p, h_p, c_p, 0, *w)
        xs, S2, h2, c2 = layer(xs, state_hgrn[l], state_rglru[l], state_conv[l], PAST_LEN, *w)
        hg_p.append(S1); lr_p.append(h1); cv_p.append(c1)
        hg_s.append(S2); lr_s.append(h2); cv_s.append(c2)
    y_prompt = rmsnorm(xp, final_norm_g)
    y_sample = rmsnorm(xs, final_norm_g)
    return (y_prompt, y_sample, jnp.stack(hg_p), jnp.stack(lr_p), jnp.stack(cv_p),
            jnp.stack(hg_s), jnp.stack(lr_s), jnp.stack(cv_s))
```

```python
import functools

import jax
import jax.numpy as jnp
from jax import lax
from jax.experimental import pallas as pl
from jax.experimental.pallas import tpu as pltpu

HG_HEADS = 8
HG_DK = 128
HG_DV = 128
HG_DIM = HG_HEADS * HG_DK
LRU_DIM = 1024
LRU_BLOCKS = 4
LRU_BW = LRU_DIM // LRU_BLOCKS
CONV_W = 4
C_LRU = 8.0
EPS = 1e-6
PAST_LEN = 4096

SUBLANES = 8
VMEM_LIMIT_BYTES = 60 * 1024 * 1024

F32 = jnp.float32
BF16 = jnp.bfloat16


def _rmsnorm(x, g):
    ms = jnp.mean(x * x, axis=-1, keepdims=True)
    return x * lax.rsqrt(ms + EPS) * g


def _dot(a, b):
    return jnp.dot(a, b, preferred_element_type=F32)


def _dot_nt(a, b):
    return lax.dot_general(a, b, (((1,), (1,)), ((), ())), preferred_element_type=F32)


def _dot_tn(a, b):
    return lax.dot_general(a, b, (((0,), (0,)), ((), ())), preferred_element_type=F32)


def _inproj_kernel(x_ref, g_ref, w_ref, z_ref, hn_ref):
    @pl.when(pl.program_id(1) == 0)
    def _():
        hn_ref[...] = _rmsnorm(x_ref[...], g_ref[...]).astype(BF16)

    z_ref[...] = _dot(hn_ref[...], w_ref[...])


def _inproj(x, g, w, tm, tn):
    m, d = x.shape
    n = w.shape[1]
    return pl.pallas_call(
        _inproj_kernel,
        grid=(m // tm, n // tn),
        in_specs=[
            pl.BlockSpec((tm, d), lambda i, j: (i, 0)),
            pl.BlockSpec((1, d), lambda i, j: (0, 0)),
            pl.BlockSpec((d, tn), lambda i, j: (0, j)),
        ],
        out_specs=pl.BlockSpec((tm, tn), lambda i, j: (i, j)),
        out_shape=jax.ShapeDtypeStruct((m, n), F32),
        scratch_shapes=[pltpu.VMEM((tm, d), BF16)],
        compiler_params=pltpu.CompilerParams(
            dimension_semantics=("arbitrary", "arbitrary"),
            vmem_limit_bytes=VMEM_LIMIT_BYTES),
        name="inproj",
    )(x, g, w)


def _bcast_block_row(x, blk, row):
    tm, w = x.shape
    if blk >= SUBLANES:
        xr = x.reshape(tm // blk, blk, w)
        return jnp.broadcast_to(xr[:, row:row + 1, :], xr.shape).reshape(tm, w)
    x8 = x.reshape(tm // SUBLANES, SUBLANES, w)
    sub = lax.broadcasted_iota(jnp.int32, x8.shape, 1)
    out = None
    for j in range(SUBLANES // blk):
        src = j * blk + row
        rj = jnp.broadcast_to(x8[:, src:src + 1, :], x8.shape)
        out = rj if out is None else jnp.where(sub >= j * blk, rj, out)
    return out.reshape(tm, w)


def _lb_for_layer(lb_raw, layer):
    if layer == 0:
        return jnp.zeros((1, lb_raw.shape[1]), F32)
    m = jnp.max(lb_raw, axis=0, keepdims=True)
    e = jnp.exp(lb_raw - m)
    sm = e / jnp.sum(e, axis=0, keepdims=True)
    return jnp.sum(sm[1:layer + 1, :], axis=0, keepdims=True)


def _softplus(x):
    return jnp.maximum(x, 0.0) + jnp.log1p(jnp.exp(-jnp.abs(x)))


def _hgrn2(zq, zf, zi, zg, lb, hg_g, seg_len, load_state, store_state, mix_ref):
    tm, w = zq.shape
    nseg = tm // seg_len
    f = lb + (1.0 - lb) * jax.nn.sigmoid(zf)
    g = jnp.log(f)
    k = 1.0 - f
    q = zq * (HG_DK ** -0.5)
    vb = zi.astype(BF16)

    pos = lax.broadcasted_iota(jnp.int32, (tm, w), 0) & (seg_len - 1)
    ri = lax.broadcasted_iota(jnp.int32, (tm, tm), 0)
    ci = lax.broadcasted_iota(jnp.int32, (tm, tm), 1)
    xr = ri ^ ci

    scores = [None] * HG_HEADS
    p = g
    h = 1
    while h < seg_len:
        tb = _bcast_block_row(p, 2 * h, h - 1)
        second = (pos & h) != 0
        e = jnp.exp(jnp.where(second, p, tb - p))
        qh = jnp.where(second, q * e, 0.0).astype(BF16)
        kh = jnp.where(second, 0.0, k * e).astype(BF16)
        upper = xr >= h
        for hd in range(HG_HEADS):
            sl = slice(hd * HG_DK, (hd + 1) * HG_DK)
            s = _dot_nt(qh[:, sl], kh[:, sl])
            scores[hd] = s if h == 1 else jnp.where(upper, s, scores[hd])
        p = jnp.where(second, p + tb, p)
        h *= 2
    tc = _bcast_block_row(p, seg_len, seg_len - 1)
    qt = (q * jnp.exp(p)).astype(BF16)
    kt = (k * jnp.exp(tc - p)).astype(BF16)
    same_seg = xr < seg_len
    qk = q * k
    silu_g = zg * jax.nn.sigmoid(zg)

    for hd in range(HG_HEADS):
        sl = slice(hd * HG_DK, (hd + 1) * HG_DK)
        a = scores[hd]
        if nseg > 1:
            a = jnp.where(same_seg, a, 0.0)
        o = _dot(a.astype(BF16), vb[:, sl])
        o = o + jnp.sum(qk[:, sl], axis=-1, keepdims=True) * zi[:, sl]
        inter = []
        for sg in range(nseg):
            rows = slice(sg * seg_len, (sg + 1) * seg_len)
            st = load_state(sg, hd)
            inter.append(_dot_nt(qt[rows, sl], st.astype(BF16)))
            decay = jnp.exp(tc[sg * seg_len:sg * seg_len + 1, sl])
            store_state(sg, hd, st * decay + _dot_tn(vb[rows, sl], kt[rows, sl]))
        o = o + (inter[0] if nseg == 1 else jnp.concatenate(inter, axis=0))
        o = o * lax.rsqrt(jnp.mean(o * o, axis=-1, keepdims=True) + EPS)
        o = o * hg_g[:, sl] * silu_g[:, sl]
        mix_ref[:, sl] = o.astype(mix_ref.dtype)


def _rglru(zx, zy, prev_rows, h0, cw, cb, wr_ref, br, wi_ref, bi, lam, seg_len, reset_mask, mix_ref):
    tm, w = zx.shape
    pos = lax.broadcasted_iota(jnp.int32, (tm, w), 0) & (seg_len - 1)
    xc = cb + cw[CONV_W - 1:CONV_W, :] * zx
    for j in range(1, CONV_W):
        shifted = jnp.where(pos >= j, pltpu.roll(zx, j, 0), pltpu.roll(prev_rows, j, 0))
        xc = xc + cw[CONV_W - 1 - j:CONV_W - j, :] * shifted
    xb = xc.astype(BF16)

    def gate(w_ref, b):
        parts = [_dot(xb[:, n * LRU_BW:(n + 1) * LRU_BW], w_ref[n]) for n in range(LRU_BLOCKS)]
        return jax.nn.sigmoid(jnp.concatenate(parts, axis=-1) + b)

    r = gate(wr_ref, br)
    i = gate(wi_ref, bi)
    log_a = -C_LRU * r * _softplus(-lam)
    a = jnp.exp(log_a)
    mult = jnp.sqrt(1.0 - a * a)
    if reset_mask is not None:
        a = jnp.where(reset_mask, 0.0, a)
        mult = jnp.where(reset_mask, 1.0, mult)
    b = xc * i * mult
    shift = 1
    while shift < seg_len:
        valid = pos >= shift
        b = jnp.where(valid, a * pltpu.roll(b, shift, 0) + b, b)
        a = jnp.where(valid, a * pltpu.roll(a, shift, 0), a)
        shift *= 2
    hseq = a * h0 + b
    mix_ref[:, HG_DIM:HG_DIM + LRU_DIM] = (hseq * jax.nn.gelu(zy)).astype(mix_ref.dtype)
    return hseq


def _split_z(z_ref):
    return [z_ref[:, n * HG_DIM:(n + 1) * HG_DIM] for n in range(6)]


def _mixer_carry_kernel(z_ref, lb_ref, hgg_ref, cw_ref, cb_ref, wr_ref, br_ref, wi_ref, bi_ref, lam_ref,
                        mix_ref, s_out, h_out, c_out, s_sc, h_sc, c_sc, *, layer, reset_first):
    i = pl.program_id(1)
    tm = z_ref.shape[0]

    @pl.when(i == 0)
    def _():
        s_sc[...] = jnp.zeros_like(s_sc)
        h_sc[...] = jnp.zeros_like(h_sc)
        c_sc[...] = jnp.zeros_like(c_sc)

    zq, zf, zi, zg, zx, zy = _split_z(z_ref)
    lb = _lb_for_layer(lb_ref[...], layer)

    def load_state(sg, hd):
        return s_sc[hd]

    def store_state(sg, hd, val):
        s_sc[hd] = val

    _hgrn2(zq, zf, zi, zg, lb, hgg_ref[...], tm, load_state, store_state, mix_ref)

    prev_rows = jnp.concatenate([jnp.zeros((tm - SUBLANES, LRU_DIM), F32), c_sc[...]], axis=0)
    reset_mask = None
    if reset_first:
        row = lax.broadcasted_iota(jnp.int32, (tm, LRU_DIM), 0)
        reset_mask = (row == 0) & (i == 0)
    hseq = _rglru(zx, zy, prev_rows, h_sc[0:1, :], cw_ref[...], cb_ref[...], wr_ref, br_ref[...],
                  wi_ref, bi_ref[...], lam_ref[...], tm, reset_mask, mix_ref)
    h_sc[...] = jnp.broadcast_to(hseq[tm - 1:tm, :], h_sc.shape)
    c_sc[...] = zx[tm - SUBLANES:tm, :]

    @pl.when(i == pl.num_programs(1) - 1)
    def _():
        s_out[0] = s_sc[...]
        h_out[0] = h_sc[...]
        c_out[0] = c_sc[...]


def _mixer_carry(z, params, batch, seq, tm, layer):
    nt = seq // tm
    lb_raw, hgg, cw, cb, wr, br, wi, bi, lam = params
    full = lambda a: pl.BlockSpec(a.shape, lambda b, i: (0,) * a.ndim)
    return pl.pallas_call(
        functools.partial(_mixer_carry_kernel, layer=layer, reset_first=True),
        grid=(batch, nt),
        in_specs=[pl.BlockSpec((tm, z.shape[1]), lambda b, i: (b * nt + i, 0))] + [full(a) for a in params],
        out_specs=[
            pl.BlockSpec((tm, HG_DIM + LRU_DIM), lambda b, i: (b * nt + i, 0)),
            pl.BlockSpec((1, HG_HEADS, HG_DV, HG_DK), lambda b, i: (b, 0, 0, 0)),
            pl.BlockSpec((1, SUBLANES, LRU_DIM), lambda b, i: (b, 0, 0)),
            pl.BlockSpec((1, SUBLANES, LRU_DIM), lambda b, i: (b, 0, 0)),
        ],
        out_shape=[
            jax.ShapeDtypeStruct((batch * seq, HG_DIM + LRU_DIM), BF16),
            jax.ShapeDtypeStruct((batch, HG_HEADS, HG_DV, HG_DK), F32),
            jax.ShapeDtypeStruct((batch, SUBLANES, LRU_DIM), F32),
            jax.ShapeDtypeStruct((batch, SUBLANES, LRU_DIM), F32),
        ],
        scratch_shapes=[
            pltpu.VMEM((HG_HEADS, HG_DV, HG_DK), F32),
            pltpu.VMEM((SUBLANES, LRU_DIM), F32),
            pltpu.VMEM((SUBLANES, LRU_DIM), F32),
        ],
        compiler_params=pltpu.CompilerParams(
            dimension_semantics=("arbitrary", "arbitrary"),
            vmem_limit_bytes=VMEM_LIMIT_BYTES),
        name="mixer_carry",
    )(z, *params)


def _mixer_seq_kernel(z_ref, lb_ref, hgg_ref, cw_ref, cb_ref, wr_ref, br_ref, wi_ref, bi_ref, lam_ref,
                      s0_ref, h0_ref, cprev_ref, mix_ref, s_out, h_out, zx_out, *, layer, seg_len):
    zq, zf, zi, zg, zx, zy = _split_z(z_ref)
    lb = _lb_for_layer(lb_ref[...], layer)

    def load_state(sg, hd):
        return s0_ref[sg, hd]

    def store_state(sg, hd, val):
        s_out[sg, hd] = val

    _hgrn2(zq, zf, zi, zg, lb, hgg_ref[...], seg_len, load_state, store_state, mix_ref)
    hseq = _rglru(zx, zy, cprev_ref[...], h0_ref[...], cw_ref[...], cb_ref[...], wr_ref, br_ref[...],
                  wi_ref, bi_ref[...], lam_ref[...], seg_len, None, mix_ref)
    h_out[...] = hseq
    zx_out[...] = zx


def _mixer_seq(z, params, s0t, h0_rows, cprev_rows, nseq, seg_len, group, layer):
    tm = group * seg_len
    full = lambda a: pl.BlockSpec(a.shape, lambda i: (0,) * a.ndim)
    rows = lambda width: pl.BlockSpec((tm, width), lambda i: (i, 0))
    st = pl.BlockSpec((group, HG_HEADS, HG_DV, HG_DK), lambda i: (i, 0, 0, 0))
    return pl.pallas_call(
        functools.partial(_mixer_seq_kernel, layer=layer, seg_len=seg_len),
        grid=(nseq // group,),
        in_specs=[rows(z.shape[1])] + [full(a) for a in params] + [st, rows(LRU_DIM), rows(LRU_DIM)],
        out_specs=[rows(HG_DIM + LRU_DIM), st, rows(LRU_DIM), rows(LRU_DIM)],
        out_shape=[
            jax.ShapeDtypeStruct((nseq * seg_len, HG_DIM + LRU_DIM), BF16),
            jax.ShapeDtypeStruct((nseq, HG_HEADS, HG_DV, HG_DK), F32),
            jax.ShapeDtypeStruct((nseq * seg_len, LRU_DIM), F32),
            jax.ShapeDtypeStruct((nseq * seg_len, LRU_DIM), F32),
        ],
        compiler_params=pltpu.CompilerParams(
            dimension_semantics=("arbitrary",),
            vmem_limit_bytes=VMEM_LIMIT_BYTES),
        name="mixer_seq",
    )(z, *params, s0t, h0_rows, cprev_rows)


def _post_kernel(x_ref, mix_ref, wo_ref, g2_ref, wg_ref, wu_ref, wd_ref, gf_ref, o_ref, h2_ref, *, final_norm):
    j = pl.program_id(1)

    @pl.when(j == 0)
    def _():
        x1 = x_ref[...] + _dot(mix_ref[...], wo_ref[...])
        o_ref[...] = x1
        h2_ref[...] = _rmsnorm(x1, g2_ref[...]).astype(BF16)

    h2 = h2_ref[...]
    gate = _dot(h2, wg_ref[...])
    up = _dot(h2, wu_ref[...])
    act = (gate * jax.nn.sigmoid(gate) * up).astype(BF16)
    o_ref[...] += _dot(act, wd_ref[...])

    if final_norm:
        @pl.when(j == pl.num_programs(1) - 1)
        def _():
            o_ref[...] = _rmsnorm(o_ref[...], gf_ref[...])


def _post(x, mix, wo, g2, wg, wu, wd, gf, tm, tf, final_norm):
    m, d = x.shape
    f = wg.shape[1]
    return pl.pallas_call(
        functools.partial(_post_kernel, final_norm=final_norm),
        grid=(m // tm, f // tf),
        in_specs=[
            pl.BlockSpec((tm, d), lambda i, j: (i, 0)),
            pl.BlockSpec((tm, mix.shape[1]), lambda i, j: (i, 0)),
            pl.BlockSpec(wo.shape, lambda i, j: (0, 0), pipeline_mode=pl.Buffered(1)),
            pl.BlockSpec((1, d), lambda i, j: (0, 0)),
            pl.BlockSpec((d, tf), lambda i, j: (0, j)),
            pl.BlockSpec((d, tf), lambda i, j: (0, j)),
            pl.BlockSpec((tf, d), lambda i, j: (j, 0)),
            pl.BlockSpec((1, d), lambda i, j: (0, 0)),
        ],
        out_specs=pl.BlockSpec((tm, d), lambda i, j: (i, 0)),
        out_shape=jax.ShapeDtypeStruct((m, d), F32),
        scratch_shapes=[pltpu.VMEM((tm, d), BF16)],
        compiler_params=pltpu.CompilerParams(
            dimension_semantics=("arbitrary", "arbitrary"),
            vmem_limit_bytes=VMEM_LIMIT_BYTES),
        name="post",
    )(x, mix, wo, g2, wg, wu, wd, gf)


def _largest_tile(n, cap):
    t = cap
    while n % t:
        t //= 2
    return t


def kernel(x_prompt, x_sample, state_hgrn, state_rglru, state_conv, norm1_g, w_in, lb_raw, hg_norm_g,
           conv_w, conv_b, w_rgate, b_rgate, w_igate, b_igate, lru_lambda, w_out, norm2_g,
           w_ffn_gate, w_ffn_up, w_ffn_down, final_norm_g):
    depth = w_in.shape[0]
    bp, tp, d = x_prompt.shape
    bs, ts, _ = x_sample.shape
    assert tp & (tp - 1) == 0 and ts & (ts - 1) == 0 and ts >= CONV_W - 1

    row = lambda a: a.reshape(1, -1)
    w_in_b, w_out_b = w_in.astype(BF16), w_out.astype(BF16)
    wg_b, wu_b, wd_b = w_ffn_gate.astype(BF16), w_ffn_up.astype(BF16), w_ffn_down.astype(BF16)
    wr_b, wi_b = w_rgate.astype(BF16), w_igate.astype(BF16)

    xp = x_prompt.reshape(bp * tp, d)
    xs = x_sample.reshape(bs * ts, d)
    tm_p = _largest_tile(bp * tp, 1024)
    tm_s = _largest_tile(bs * ts, 1024)
    tn = _largest_tile(w_in.shape[2], 1024)
    tf = _largest_tile(w_ffn_gate.shape[2], 512)
    tmix = min(tp, 256)
    group = _largest_tile(bs, 8)

    s0t = jnp.swapaxes(state_hgrn, -1, -2)
    h0_rows = jnp.repeat(state_rglru, ts, axis=1)
    hist = state_conv.reshape(depth, bs // group, group, CONV_W - 1, LRU_DIM)
    hist = jnp.roll(hist, -1, axis=2).reshape(depth, bs, CONV_W - 1, LRU_DIM)
    cprev = jnp.zeros((depth, bs, ts, LRU_DIM), F32)
    cprev = cprev.at[:, :, ts - (CONV_W - 1):, :].set(hist)
    cprev = cprev.reshape(depth, bs * ts, LRU_DIM)

    outs = {k: [] for k in ("hg_p", "lr_p", "cv_p", "hg_s", "lr_s", "cv_s")}
    for l in range(depth):
        params = (lb_raw, row(hg_norm_g[l]), conv_w[l], row(conv_b[l]), wr_b[l], row(b_rgate[l]),
                  wi_b[l], row(b_igate[l]), row(lru_lambda[l]))
        last = l == depth - 1

        zp = _inproj(xp, row(norm1_g[l]), w_in_b[l], tm_p, tn)
        mix_p, s_p, h_p, c_p = _mixer_carry(zp, params, bp, tp, tmix, l)
        xp = _post(xp, mix_p, w_out_b[l], row(norm2_g[l]), wg_b[l], wu_b[l], wd_b[l], row(final_norm_g),
                   _largest_tile(bp * tp, 512), tf, last)

        zs = _inproj(xs, row(norm1_g[l]), w_in_b[l], tm_s, tn)
        mix_s, s_s, h_s, zx_s = _mixer_seq(zs, params, s0t[l], h0_rows[l], cprev[l], bs, ts, group, l)
        xs = _post(xs, mix_s, w_out_b[l], row(norm2_g[l]), wg_b[l], wu_b[l], wd_b[l], row(final_norm_g),
                   _largest_tile(bs * ts, 512), tf, last)

        outs["hg_p"].append(jnp.swapaxes(s_p, -1, -2))
        outs["lr_p"].append(h_p[:, 0, :])
        outs["cv_p"].append(c_p[:, SUBLANES - (CONV_W - 1):, :])
        outs["hg_s"].append(jnp.swapaxes(s_s, -1, -2))
        outs["lr_s"].append(h_s.reshape(bs, ts, LRU_DIM)[:, ts - 1, :])
        outs["cv_s"].append(zx_s.reshape(bs, ts, LRU_DIM)[:, ts - (CONV_W - 1):, :])

    return (xp.reshape(bp, tp, d), xs.reshape(bs, ts, d),
            jnp.stack(outs["hg_p"]), jnp.stack(outs["lr_p"]), jnp.stack(outs["cv_p"]),
            jnp.stack(outs["hg_s"]), jnp.stack(outs["lr_s"]), jnp.stack(outs["cv_s"]))
```

```python
import functools

import jax
import jax.numpy as jnp
from jax import lax
from jax.experimental import pallas as pl
from jax.experimental.pallas import tpu as pltpu

HG_HEADS = 8
HG_DK = 128
HG_DV = 128
HG_DIM = HG_HEADS * HG_DK
LRU_DIM = 1024
LRU_BLOCKS = 4
LRU_BW = LRU_DIM // LRU_BLOCKS
CONV_W = 4
C_LRU = 8.0
EPS = 1e-6
HG_COLS = 4 * HG_DIM
LRU_COLS = 2 * LRU_DIM
PROJ_CHUNK = 256
HGRN_PLAN = (3, 1, 0, 1, 0, 1, 0, 1, 0, 2)
LRU_PLAN = (3, 1, 2, 1, 1, 2, 1, 1, 2, 1, 1)

SUBLANES = 8
VMEM_LIMIT_BYTES = 60 * 1024 * 1024

F32 = jnp.float32
BF16 = jnp.bfloat16


def _rmsnorm(x, g):
    ms = jnp.mean(x * x, axis=-1, keepdims=True)
    return x * lax.rsqrt(ms + EPS) * g


def _dot(a, b):
    return jnp.dot(a, b, preferred_element_type=F32)


def _dot_nt(a, b):
    return lax.dot_general(a, b, (((1,), (1,)), ((), ())), preferred_element_type=F32)


def _dot_tn(a, b):
    return lax.dot_general(a, b, (((0,), (0,)), ((), ())), preferred_element_type=F32)


def _layer_block(arr, layer, grid_rank, **kw):
    zeros = (0,) * (arr.ndim - 1)
    return pl.BlockSpec((None,) + arr.shape[1:], lambda *_: (layer,) + zeros, **kw)


def _whole(arr):
    zeros = (0,) * arr.ndim
    return pl.BlockSpec(arr.shape, lambda *_: zeros)


def _inproj_kernel(x_ref, g_ref, w_ref, z_ref, hn_ref):
    @pl.when(pl.program_id(1) == 0)
    def _():
        hn_ref[...] = _rmsnorm(x_ref[...], g_ref[...]).astype(BF16)

    z_ref[...] = _dot(hn_ref[...], w_ref[...])


def _inproj(x, g, w, layer, tm, tn):
    m, d = x.shape
    n = w.shape[2]
    return pl.pallas_call(
        _inproj_kernel,
        grid=(m // tm, n // tn),
        in_specs=[
            pl.BlockSpec((tm, d), lambda i, j: (i, 0)),
            _layer_block(g, layer, 2),
            pl.BlockSpec((None, d, tn), lambda i, j: (layer, 0, j)),
        ],
        out_specs=pl.BlockSpec((tm, tn), lambda i, j: (i, j)),
        out_shape=jax.ShapeDtypeStruct((m, n), F32),
        scratch_shapes=[pltpu.VMEM((tm, d), BF16)],
        compiler_params=pltpu.CompilerParams(
            dimension_semantics=("arbitrary", "arbitrary"),
            vmem_limit_bytes=VMEM_LIMIT_BYTES),
        name="inproj",
    )(x, g, w)


def _bcast_block_row(x, blk, row):
    tm, w = x.shape
    if blk >= SUBLANES:
        xr = x.reshape(tm // blk, blk, w)
        return jnp.broadcast_to(xr[:, row:row + 1, :], xr.shape).reshape(tm, w)
    x8 = x.reshape(tm // SUBLANES, SUBLANES, w)
    sub = lax.broadcasted_iota(jnp.int32, x8.shape, 1)
    out = None
    for j in range(SUBLANES // blk):
        src = j * blk + row
        rj = jnp.broadcast_to(x8[:, src:src + 1, :], x8.shape)
        out = rj if out is None else jnp.where(sub >= j * blk, rj, out)
    return out.reshape(tm, w)


def _lb_for_layer(lb_raw, layer):
    if layer == 0:
        return jnp.zeros((1, lb_raw.shape[1]), F32)
    m = jnp.max(lb_raw, axis=0, keepdims=True)
    e = jnp.exp(lb_raw - m)
    sm = e / jnp.sum(e, axis=0, keepdims=True)
    return jnp.sum(sm[1:layer + 1, :], axis=0, keepdims=True)


def _softplus(x):
    return jnp.maximum(x, 0.0) + jnp.log1p(jnp.exp(-jnp.abs(x)))


def _hgrn2(load_z, lb, hg_g, seg_len, load_state, store_state, mix_ref):
    zq, zf, zi, zg = load_z()
    tm, w = zq.shape
    nseg = tm // seg_len
    f = lb + (1.0 - lb) * jax.nn.sigmoid(zf)
    g = jnp.log(f)
    k = 1.0 - f
    q = zq * (HG_DK ** -0.5)
    vb = zi.astype(BF16)

    pos = lax.broadcasted_iota(jnp.int32, (tm, w), 0) & (seg_len - 1)
    ri = lax.broadcasted_iota(jnp.int32, (tm, tm), 0)
    ci = lax.broadcasted_iota(jnp.int32, (tm, tm), 1)
    xr = ri ^ ci
    yield

    scores = [None] * HG_HEADS
    p = g
    h = 1
    while h < seg_len:
        tb = _bcast_block_row(p, 2 * h, h - 1)
        second = (pos & h) != 0
        e = jnp.exp(jnp.where(second, p, tb - p))
        qh = jnp.where(second, q * e, 0.0).astype(BF16)
        kh = jnp.where(second, 0.0, k * e).astype(BF16)
        upper = xr >= h
        for hd in range(HG_HEADS):
            sl = slice(hd * HG_DK, (hd + 1) * HG_DK)
            s = _dot_nt(qh[:, sl], kh[:, sl])
            scores[hd] = s if h == 1 else jnp.where(upper, s, scores[hd])
        p = jnp.where(second, p + tb, p)
        h *= 2
        yield
    tc = _bcast_block_row(p, seg_len, seg_len - 1)
    qt = (q * jnp.exp(p)).astype(BF16)
    kt = (k * jnp.exp(tc - p)).astype(BF16)
    same_seg = xr < seg_len
    qk = q * k
    silu_g = zg * jax.nn.sigmoid(zg)
    yield

    for hd in range(HG_HEADS):
        sl = slice(hd * HG_DK, (hd + 1) * HG_DK)
        a = scores[hd]
        if nseg > 1:
            a = jnp.where(same_seg, a, 0.0)
        o = _dot(a.astype(BF16), vb[:, sl])
        o = o + jnp.sum(qk[:, sl], axis=-1, keepdims=True) * zi[:, sl]
        inter = []
        for sg in range(nseg):
            rows = slice(sg * seg_len, (sg + 1) * seg_len)
            st = load_state(sg, hd)
            inter.append(_dot_nt(qt[rows, sl], st.astype(BF16)))
            decay = jnp.exp(tc[sg * seg_len:sg * seg_len + 1, sl])
            store_state(sg, hd, st * decay + _dot_tn(vb[rows, sl], kt[rows, sl]))
        o = o + (inter[0] if nseg == 1 else jnp.concatenate(inter, axis=0))
        o = o * lax.rsqrt(jnp.mean(o * o, axis=-1, keepdims=True) + EPS)
        o = o * hg_g[:, sl] * silu_g[:, sl]
        mix_ref[:, sl] = o.astype(mix_ref.dtype)
        yield


def _rglru(load_z, cw, cb, wr_ref, br, wi_ref, bi, lam, seg_len, reset_mask, mix_ref, done):
    zx, zy, prev_rows, h0 = load_z()
    tm, w = zx.shape
    pos = lax.broadcasted_iota(jnp.int32, (tm, w), 0) & (seg_len - 1)
    xc = cb + cw[CONV_W - 1:CONV_W, :] * zx
    for j in range(1, CONV_W):
        shifted = jnp.where(pos >= j, pltpu.roll(zx, j, 0), pltpu.roll(prev_rows, j, 0))
        xc = xc + cw[CONV_W - 1 - j:CONV_W - j, :] * shifted
    xb = xc.astype(BF16)
    gelu_y = jax.nn.gelu(zy)
    yield

    def gate(w_ref, b):
        parts = [_dot(xb[:, n * LRU_BW:(n + 1) * LRU_BW], w_ref[n]) for n in range(LRU_BLOCKS)]
        return jax.nn.sigmoid(jnp.concatenate(parts, axis=-1) + b)

    r = gate(wr_ref, br)
    yield
    i = gate(wi_ref, bi)
    log_a = -C_LRU * r * _softplus(-lam)
    a = jnp.exp(log_a)
    mult = jnp.sqrt(1.0 - a * a)
    if reset_mask is not None:
        a = jnp.where(reset_mask, 0.0, a)
        mult = jnp.where(reset_mask, 1.0, mult)
    b = xc * i * mult
    yield
    shift = 1
    while shift < seg_len:
        valid = pos >= shift
        b = jnp.where(valid, a * pltpu.roll(b, shift, 0) + b, b)
        a = jnp.where(valid, a * pltpu.roll(a, shift, 0), a)
        shift *= 2
        yield
    hseq = a * h0 + b
    mix_ref[:, HG_DIM:HG_DIM + LRU_DIM] = (hseq * gelu_y).astype(mix_ref.dtype)
    done(hseq, zx)


def _plan_at(plan, step):
    return plan[step] if step < len(plan) else 0


def _blocks(ref, n):
    return [ref[:, j * HG_DIM:(j + 1) * HG_DIM] for j in range(n)]


def _mixer_fused_kernel(hn_ref, w_ref, lb_ref, hgg_ref, cw_ref, cb_ref, wr_ref, br_ref,
                        wi_ref, bi_ref, lam_ref, mix_ref, s_out, h_out, c_out,
                        s_sc, h_sc, c_sc, *, layer, tiles_per_seq):
    t = pl.program_id(0)
    i = lax.rem(t, tiles_per_seq)
    tm = hn_ref.shape[0]

    def project(block):
        parts = []
        for j in range(HG_DIM // PROJ_CHUNK):
            c0 = block * HG_DIM + j * PROJ_CHUNK
            parts.append(_dot(hn_ref[...], w_ref[:, c0:c0 + PROJ_CHUNK]))
        return jnp.concatenate(parts, axis=-1)

    @pl.when(i == 0)
    def _():
        s_sc[...] = jnp.zeros_like(s_sc)
        h_sc[...] = jnp.zeros_like(h_sc)
        c_sc[...] = jnp.zeros_like(c_sc)

    lb = _lb_for_layer(lb_ref[...], layer)

    def load_state(sg, hd):
        return s_sc[hd]

    def store_state(sg, hd, val):
        s_sc[hd] = val

    def load_lru():
        prev_rows = jnp.concatenate([jnp.zeros((tm - SUBLANES, LRU_DIM), F32), c_sc[...]], axis=0)
        return [project(4), project(5), prev_rows, h_sc[0:1, :]]

    def done(hseq, zx):
        h_sc[...] = jnp.broadcast_to(hseq[tm - 1:tm, :], h_sc.shape)
        c_sc[...] = zx[tm - SUBLANES:tm, :]

    row = lax.broadcasted_iota(jnp.int32, (tm, LRU_DIM), 0)
    reset_mask = (row == 0) & (i == 0)
    for _ in _rglru(load_lru, cw_ref[...], cb_ref[...], wr_ref, br_ref[...], wi_ref, bi_ref[...],
                    lam_ref[...], tm, reset_mask, mix_ref, done):
        pass
    for _ in _hgrn2(lambda: [project(n) for n in range(4)], lb, hgg_ref[...], tm, load_state,
                    store_state, mix_ref):
        pass

    @pl.when(i == tiles_per_seq - 1)
    def _():
        s_out[0] = s_sc[...]
        h_out[0] = h_sc[...]
        c_out[0] = c_sc[...]


def _mixer_fused(hn, w_in, params, layer, batch, seq, tm):
    nt = seq // tm
    ntiles = batch * nt
    d = hn.shape[1]
    lb_raw, per_layer = params[0], params[1:]
    return pl.pallas_call(
        functools.partial(_mixer_fused_kernel, layer=layer, tiles_per_seq=nt),
        grid=(ntiles,),
        in_specs=[
            pl.BlockSpec((tm, d), lambda t: (t, 0)),
            _layer_block(w_in, layer, 1, pipeline_mode=pl.Buffered(1)),
            _whole(lb_raw),
        ] + [_layer_block(a, layer, 1) for a in per_layer],
        out_specs=[
            pl.BlockSpec((tm, HG_DIM + LRU_DIM), lambda t: (t, 0)),
            pl.BlockSpec((1, HG_HEADS, HG_DV, HG_DK), lambda t: (t // nt, 0, 0, 0)),
            pl.BlockSpec((1, SUBLANES, LRU_DIM), lambda t: (t // nt, 0, 0)),
            pl.BlockSpec((1, SUBLANES, LRU_DIM), lambda t: (t // nt, 0, 0)),
        ],
        out_shape=[
            jax.ShapeDtypeStruct((batch * seq, HG_DIM + LRU_DIM), BF16),
            jax.ShapeDtypeStruct((batch, HG_HEADS, HG_DV, HG_DK), F32),
            jax.ShapeDtypeStruct((batch, SUBLANES, LRU_DIM), F32),
            jax.ShapeDtypeStruct((batch, SUBLANES, LRU_DIM), F32),
        ],
        scratch_shapes=[
            pltpu.VMEM((HG_HEADS, HG_DV, HG_DK), F32),
            pltpu.VMEM((SUBLANES, LRU_DIM), F32),
            pltpu.VMEM((SUBLANES, LRU_DIM), F32),
        ],
        compiler_params=pltpu.CompilerParams(
            dimension_semantics=("arbitrary",),
            vmem_limit_bytes=VMEM_LIMIT_BYTES),
        name="mixer_fused",
    )(hn, w_in, *params)


def _mixer_seq_kernel(z_ref, lb_ref, hgg_ref, cw_ref, cb_ref, wr_ref, br_ref, wi_ref, bi_ref, lam_ref,
                      s0_ref, h0_ref, cprev_ref, mix_ref, s_out, h_out, zx_out, *, layer, seg_len):
    lb = _lb_for_layer(lb_ref[...], layer)

    def load_state(sg, hd):
        return s0_ref[sg, hd]

    def store_state(sg, hd, val):
        s_out[sg, hd] = val

    def done(hseq, zx):
        h_out[...] = hseq
        zx_out[...] = zx

    for _ in _hgrn2(lambda: _blocks(z_ref, 4), lb, hgg_ref[...], seg_len, load_state, store_state, mix_ref):
        pass
    for _ in _rglru(lambda: _blocks(z_ref, 6)[4:] + [cprev_ref[...], h0_ref[...]], cw_ref[...], cb_ref[...],
                    wr_ref, br_ref[...], wi_ref, bi_ref[...], lam_ref[...], seg_len, None, mix_ref, done):
        pass


def _mixer_seq(z, params, s0t, h0_rows, cprev_rows, layer, nseq, seg_len, group):
    tm = group * seg_len
    lb_raw, per_layer = params[0], params[1:]
    rows = lambda width: pl.BlockSpec((tm, width), lambda i: (i, 0))
    lrows = lambda width: pl.BlockSpec((None, tm, width), lambda i: (layer, i, 0))
    st_in = pl.BlockSpec((None, group, HG_HEADS, HG_DV, HG_DK), lambda i: (layer, i, 0, 0, 0))
    st_out = pl.BlockSpec((group, HG_HEADS, HG_DV, HG_DK), lambda i: (i, 0, 0, 0))
    return pl.pallas_call(
        functools.partial(_mixer_seq_kernel, layer=layer, seg_len=seg_len),
        grid=(nseq // group,),
        in_specs=[rows(z.shape[1]), _whole(lb_raw)] + [_layer_block(a, layer, 1) for a in per_layer]
        + [st_in, lrows(LRU_DIM), lrows(LRU_DIM)],
        out_specs=[rows(HG_DIM + LRU_DIM), st_out, rows(LRU_DIM), rows(LRU_DIM)],
        out_shape=[
            jax.ShapeDtypeStruct((nseq * seg_len, HG_DIM + LRU_DIM), BF16),
            jax.ShapeDtypeStruct((nseq, HG_HEADS, HG_DV, HG_DK), F32),
            jax.ShapeDtypeStruct((nseq * seg_len, LRU_DIM), F32),
            jax.ShapeDtypeStruct((nseq * seg_len, LRU_DIM), F32),
        ],
        compiler_params=pltpu.CompilerParams(
            dimension_semantics=("arbitrary",),
            vmem_limit_bytes=VMEM_LIMIT_BYTES),
        name="mixer_seq",
    )(z, *params, s0t, h0_rows, cprev_rows)


def _post_kernel(x_ref, mix_ref, wo_ref, g2_ref, wg_ref, wu_ref, wd_ref, gn_ref, o_ref, *rest, tail):
    hn_ref, h2_ref = rest if tail == "hn" else (None,) + rest
    j = pl.program_id(1)

    @pl.when(j == 0)
    def _():
        x1 = x_ref[...] + _dot(mix_ref[...], wo_ref[...])
        o_ref[...] = x1
        h2_ref[...] = _rmsnorm(x1, g2_ref[...]).astype(BF16)

    h2 = h2_ref[...]
    gate = _dot(h2, wg_ref[...])
    up = _dot(h2, wu_ref[...])
    act = (gate * jax.nn.sigmoid(gate) * up).astype(BF16)
    o_ref[...] += _dot(act, wd_ref[...])

    if tail != "none":
        @pl.when(j == pl.num_programs(1) - 1)
        def _():
            y = _rmsnorm(o_ref[...], gn_ref[...])
            if tail == "final":
                o_ref[...] = y
            else:
                hn_ref[...] = y.astype(BF16)


def _post(x, mix, wo, g2, wg, wu, wd, gn, layer, tm, tf, tail):
    m, d = x.shape
    f = wg.shape[2]
    row_block = pl.BlockSpec((tm, d), lambda i, j: (i, 0))
    out_specs, out_shape = row_block, jax.ShapeDtypeStruct((m, d), F32)
    if tail == "hn":
        out_specs, out_shape = [row_block, row_block], [out_shape, jax.ShapeDtypeStruct((m, d), BF16)]
    return pl.pallas_call(
        functools.partial(_post_kernel, tail=tail),
        grid=(m // tm, f // tf),
        in_specs=[
            pl.BlockSpec((tm, d), lambda i, j: (i, 0)),
            pl.BlockSpec((tm, mix.shape[1]), lambda i, j: (i, 0)),
            _layer_block(wo, layer, 2, pipeline_mode=pl.Buffered(1)),
            _layer_block(g2, layer, 2),
            pl.BlockSpec((None, d, tf), lambda i, j: (layer, 0, j)),
            pl.BlockSpec((None, d, tf), lambda i, j: (layer, 0, j)),
            pl.BlockSpec((None, tf, d), lambda i, j: (layer, j, 0)),
            _layer_block(gn, layer, 2),
        ],
        out_specs=out_specs,
        out_shape=out_shape,
        scratch_shapes=[pltpu.VMEM((tm, d), BF16)],
        compiler_params=pltpu.CompilerParams(
            dimension_semantics=("arbitrary", "arbitrary"),
            vmem_limit_bytes=VMEM_LIMIT_BYTES),
        name="post",
    )(x, mix, wo, g2, wg, wu, wd, gn)


def _norm_kernel(x_ref, g_ref, o_ref):
    o_ref[...] = _rmsnorm(x_ref[...], g_ref[...]).astype(o_ref.dtype)


def _norm_cast(x, g, layer, tm):
    m, d = x.shape
    return pl.pallas_call(
        _norm_kernel,
        grid=(m // tm,),
        in_specs=[pl.BlockSpec((tm, d), lambda i: (i, 0)), _layer_block(g, layer, 1)],
        out_specs=pl.BlockSpec((tm, d), lambda i: (i, 0)),
        out_shape=jax.ShapeDtypeStruct((m, d), BF16),
        compiler_params=pltpu.CompilerParams(
            dimension_semantics=("arbitrary",),
            vmem_limit_bytes=VMEM_LIMIT_BYTES),
        name="norm_cast",
    )(x, g)


def _largest_tile(n, cap):
    t = cap
    while n % t:
        t //= 2
    return t


def kernel(x_prompt, x_sample, state_hgrn, state_rglru, state_conv, norm1_g, w_in, lb_raw, hg_norm_g,
           conv_w, conv_b, w_rgate, b_rgate, w_igate, b_igate, lru_lambda, w_out, norm2_g,
           w_ffn_gate, w_ffn_up, w_ffn_down, final_norm_g):
    depth = w_in.shape[0]
    bp, tp, d = x_prompt.shape
    bs, ts, _ = x_sample.shape
    assert tp & (tp - 1) == 0 and ts & (ts - 1) == 0 and ts >= CONV_W - 1

    rows = lambda a: a.reshape(depth, 1, -1)
    w_in_b, w_out_b = w_in.astype(BF16), w_out.astype(BF16)
    wg_b, wu_b, wd_b = w_ffn_gate.astype(BF16), w_ffn_up.astype(BF16), w_ffn_down.astype(BF16)
    g1, g2, gf = rows(norm1_g), rows(norm2_g), final_norm_g.reshape(1, d)
    params = (lb_raw, rows(hg_norm_g), conv_w, rows(conv_b), w_rgate.astype(BF16), rows(b_rgate),
              w_igate.astype(BF16), rows(b_igate), rows(lru_lambda))

    xp = x_prompt.reshape(bp * tp, d)
    xs = x_sample.reshape(bs * ts, d)
    tm_s = _largest_tile(bs * ts, 1024)
    tn = _largest_tile(w_in.shape[2], 1024)
    tf = _largest_tile(w_ffn_gate.shape[2], 512)
    tmix = min(tp, 256)
    group = _largest_tile(bs, 8)

    s0t = jnp.swapaxes(state_hgrn, -1, -2)
    h0_rows = jnp.repeat(state_rglru, ts, axis=1)
    hist = state_conv.reshape(depth, bs // group, group, CONV_W - 1, LRU_DIM)
    hist = jnp.roll(hist, -1, axis=2).reshape(depth, bs, CONV_W - 1, LRU_DIM)
    cprev = jnp.zeros((depth, bs, ts, LRU_DIM), F32)
    cprev = cprev.at[:, :, ts - (CONV_W - 1):, :].set(hist)
    cprev = cprev.reshape(depth, bs * ts, LRU_DIM)

    g_next = jnp.concatenate([g1[1:], gf.reshape(1, 1, d)], axis=0)
    tm_post_p, tm_post_s = _largest_tile(bp * tp, 512), _largest_tile(bs * ts, 512)

    outs = {k: [] for k in ("hg_p", "lr_p", "cv_p", "hg_s", "lr_s", "cv_s")}
    hn_p = _norm_cast(xp, g1, 0, _largest_tile(bp * tp, 1024))
    for l in range(depth):
        last = l == depth - 1

        mix_p, s_p, h_p, c_p = _mixer_fused(hn_p, w_in_b, params, l, bp, tp, tmix)
        res = _post(xp, mix_p, w_out_b, g2, wg_b, wu_b, wd_b, g_next, l, tm_post_p, tf,
                    "final" if last else "hn")
        xp, hn_p = (res, None) if last else res

        zs = _inproj(xs, g1, w_in_b, l, tm_s, tn)
        mix_s, s_s, h_s, zx_s = _mixer_seq(zs, params, s0t, h0_rows, cprev, l, bs, ts, group)
        xs = _post(xs, mix_s, w_out_b, g2, wg_b, wu_b, wd_b, g_next, l, tm_post_s, tf,
                   "final" if last else "none")

        outs["hg_p"].append(s_p)
        outs["lr_p"].append(h_p[:, 0, :])
        outs["cv_p"].append(c_p[:, SUBLANES - (CONV_W - 1):, :])
        outs["hg_s"].append(s_s)
        outs["lr_s"].append(h_s.reshape(bs, ts, LRU_DIM)[:, ts - 1, :])
        outs["cv_s"].append(zx_s.reshape(bs, ts, LRU_DIM)[:, ts - (CONV_W - 1):, :])

    return (xp.reshape(bp, tp, d), xs.reshape(bs, ts, d),
            jnp.swapaxes(jnp.stack(outs["hg_p"]), -1, -2), jnp.stack(outs["lr_p"]), jnp.stack(outs["cv_p"]),
            jnp.swapaxes(jnp.stack(outs["hg_s"]), -1, -2), jnp.stack(outs["lr_s"]), jnp.stack(outs["cv_s"]))
```
